```python
import jax, jax.numpy as jnp
from jax import lax
import numpy as np


D_MODEL = 1024
BATCH = 8
SEQ = 2048
DEPTH = 2

HEAD_DIM = 64
ROPE_THETA = 10000.0
LN_EPS = 1e-5
A_HEADS = 8
IDX_HEADS = 8
IDX_DIM = 32
DSA_TOPK = 256
DSA_QBLOCK = 128
B_HEADS = 8
B_GROUPS = ((128, 1), (512, 4), (2048, 16))
B_QBLOCK = 128
C_HEADS = 8
MOBA_BLOCK = 256
MOBA_TOPK = 3
MOBA_QCHUNK = 32
N_EXPERTS = 64
N_GROUPS = 8
TOPK_GROUPS = 4
TOP_K = 8
EXPERT_DIM = 256
SHARED_DIM = 256
ROUTED_SCALE = 2.5
ALPHA = (2 * DEPTH) ** 0.25
BETA = (8 * DEPTH) ** -0.25
IN_WIDTHS = (A_HEADS * HEAD_DIM, HEAD_DIM, HEAD_DIM, IDX_HEADS * IDX_DIM, IDX_DIM, IDX_HEADS,
             B_HEADS * HEAD_DIM, B_HEADS * HEAD_DIM, B_HEADS * HEAD_DIM,
             C_HEADS * HEAD_DIM, C_HEADS * HEAD_DIM, C_HEADS * HEAD_DIM,
             D_MODEL, D_MODEL, D_MODEL)
IN_WIDTH = sum(IN_WIDTHS)

kernel_name = 'hybrid_dsa_dilated_moba_moe_block'

F32 = jnp.float32


def layer_norm(x, g, b):
    xf = x.astype(F32)
    mu = xf.mean(-1, keepdims=True)
    var = jnp.mean(jnp.square(xf - mu), -1, keepdims=True)
    return ((xf - mu) * lax.rsqrt(var + LN_EPS) * g.astype(F32) + b.astype(F32)).astype(x.dtype)


def rope_tables(seq, dim):
    inv = ROPE_THETA ** (-jnp.arange(0, dim, 2, dtype=F32) / dim)
    ang = jnp.arange(seq, dtype=F32)[:, None] * inv[None, :]
    return jnp.cos(ang), jnp.sin(ang)


def apply_rope(x, cos, sin):
    extra = x.ndim - 3
    cos = cos.reshape(cos.shape[:1] + (1,) * extra + cos.shape[1:])
    sin = sin.reshape(sin.shape[:1] + (1,) * extra + sin.shape[1:])
    x1, x2 = jnp.split(x.astype(F32), 2, axis=-1)
    return jnp.concatenate([x1 * cos - x2 * sin, x2 * cos + x1 * sin], -1).astype(x.dtype)


def dsa_attention(q, k, v, qi, ki, wi):
    bsz, S, _, dh = q.shape
    topk = min(DSA_TOPK, S // 4)
    kpos = jnp.arange(S)
    def block(i):
        t0 = i * DSA_QBLOCK
        tq = t0 + jnp.arange(DSA_QBLOCK)
        qb = lax.dynamic_slice_in_dim(q, t0, DSA_QBLOCK, axis=1)
        qib = lax.dynamic_slice_in_dim(qi, t0, DSA_QBLOCK, axis=1)
        wib = lax.dynamic_slice_in_dim(wi, t0, DSA_QBLOCK, axis=1).astype(F32) * IDX_HEADS ** -0.5
        dots = jnp.einsum('bqhe,bse->bhqs', qib, ki, preferred_element_type=F32) * IDX_DIM ** -0.5
        score = jnp.einsum('bqh,bhqs->bqs', wib, jax.nn.relu(dots))
        causal = kpos[None, :] <= tq[:, None]
        score = jnp.where(causal[None], score, -jnp.inf)
        _, idx = lax.top_k(score, topk)
        valid = idx <= tq[None, :, None]
        kg = jax.vmap(lambda kk, ii: kk[ii])(k, idx)
        vg = jax.vmap(lambda vv, ii: vv[ii])(v, idx)
        logits = jnp.einsum('bqhd,bqkd->bhqk', qb, kg, preferred_element_type=F32) * dh ** -0.5
        logits = jnp.where(valid[:, None], logits, -jnp.inf)
        p = jax.nn.softmax(logits, axis=-1).astype(v.dtype)
        return jnp.einsum('bhqk,bqkd->bqhd', p, vg)
    out = lax.map(block, jnp.arange(S // DSA_QBLOCK))
    return jnp.moveaxis(out, 0, 1).reshape(bsz, S, -1)


def dilated_attention(q, k, v):
    bsz, S, H, dh = q.shape
    scale = dh ** -0.5
    def block(i):
        t0 = i * B_QBLOCK
        tq = t0 + jnp.arange(B_QBLOCK)
        qb = lax.dynamic_slice_in_dim(q, t0, B_QBLOCK, axis=1)
        ms, ls, outs = [], [], []
        for window, dil in B_GROUPS:
            n_keys = window // dil + 1
            kpos = tq[:, None] - dil * jnp.arange(n_keys)[None, :]
            valid = kpos >= 0
            kpos = jnp.maximum(kpos, 0)
            kg = k[:, kpos]
            vg = v[:, kpos]
            logits = jnp.einsum('bqhd,bqnhd->bhqn', qb, kg, preferred_element_type=F32) * scale
            logits = jnp.where(valid[None, None], logits, -jnp.inf)
            m = logits.max(-1)
            e = jnp.exp(logits - m[..., None])
            l = e.sum(-1)
            o = jnp.einsum('bhqn,bqnhd->bhqd', (e / l[..., None]).astype(v.dtype), vg,
                           preferred_element_type=F32)
            ms.append(m); ls.append(l); outs.append(o)
        m_all, l_all, o_all = jnp.stack(ms), jnp.stack(ls), jnp.stack(outs)
        wgt = l_all * jnp.exp(m_all - m_all.max(0))
        o = (wgt[..., None] * o_all).sum(0) / wgt.sum(0)[..., None]
        return o.transpose(0, 2, 1, 3).astype(q.dtype)
    out = lax.map(block, jnp.arange(S // B_QBLOCK))
    return jnp.moveaxis(out, 0, 1).reshape(bsz, S, -1)


def moba_attention(q, k, v):
    bsz, S, H, dh = q.shape
    scale = dh ** -0.5
    n_blk = -(-S // MOBA_BLOCK)
    pad = n_blk * MOBA_BLOCK - S
    k_pad = jnp.pad(k, ((0, 0), (0, pad), (0, 0), (0, 0)))
    v_pad = jnp.pad(v, ((0, 0), (0, pad), (0, 0), (0, 0)))
    k_blk = k_pad.reshape(bsz, n_blk, MOBA_BLOCK, H, dh).transpose(0, 3, 1, 2, 4)
    v_blk = v_pad.reshape(bsz, n_blk, MOBA_BLOCK, H, dh).transpose(0, 3, 1, 2, 4)
    k_mean = k_blk.astype(F32).mean(3)
    n_sel = max(1, min(MOBA_TOPK, n_blk - 1))
    b_ix = jnp.arange(bsz)[:, None, None, None]
    h_ix = jnp.arange(H)[None, :, None, None]
    blk_ids = jnp.arange(n_blk)
    def chunk(i):
        t0 = i * MOBA_QCHUNK
        tq = t0 + jnp.arange(MOBA_QCHUNK)
        own = t0 // MOBA_BLOCK
        qc = lax.dynamic_slice_in_dim(q, t0, MOBA_QCHUNK, axis=1)
        gate = jnp.einsum('bqhd,bhnd->bhqn', qc.astype(F32), k_mean)
        gate = jnp.where(blk_ids < own, gate, -jnp.inf)
        _, sel = lax.top_k(gate, n_sel)
        sel_ok = sel < own
        kg = k_blk[b_ix, h_ix, sel]
        vg = v_blk[b_ix, h_ix, sel]
        lp = jnp.einsum('bqhd,bhqkjd->bhqkj', qc, kg, preferred_element_type=F32) * scale
        lp = jnp.where(sel_ok[..., None], lp, -jnp.inf).reshape(bsz, H, MOBA_QCHUNK, n_sel * MOBA_BLOCK)
        start = own * MOBA_BLOCK
        k_own = lax.dynamic_slice_in_dim(k_pad, start, MOBA_BLOCK, axis=1)
        v_own = lax.dynamic_slice_in_dim(v_pad, start, MOBA_BLOCK, axis=1)
        lo = jnp.einsum('bqhd,bjhd->bhqj', qc, k_own, preferred_element_type=F32) * scale
        causal = (start + jnp.arange(MOBA_BLOCK))[None, :] <= tq[:, None]
        lo = jnp.where(causal, lo, -jnp.inf)
        p = jax.nn.softmax(jnp.concatenate([lp, lo], -1), axis=-1).astype(v.dtype)
        p_sel = p[..., :n_sel * MOBA_BLOCK].reshape(bsz, H, MOBA_QCHUNK, n_sel, MOBA_BLOCK)
        p_own = p[..., n_sel * MOBA_BLOCK:]
        return (jnp.einsum('bhqkj,bhqkjd->bqhd', p_sel, vg)
                + jnp.einsum('bhqj,bjhd->bqhd', p_own, v_own))
    out = lax.map(chunk, jnp.arange(S // MOBA_QCHUNK))
    return jnp.moveaxis(out, 0, 1).reshape(bsz, S, -1)


def token_mixing(u, w_in, ik_g, ik_b, w_br_a, w_br_b, w_br_c, w_out):
    bsz, S, _ = u.shape
    cos_h, sin_h = rope_tables(S, HEAD_DIM)
    cos_i, sin_i = rope_tables(S, IDX_DIM)
    proj = jnp.einsum('bsd,dn->bsn', u, w_in)
    parts = jnp.split(proj, np.cumsum(IN_WIDTHS)[:-1].tolist(), axis=-1)
    qa, ka, va, qi, ki, wi, qb, kb, vb, qc, kc, vc, ga, gb, gc = parts
    heads = lambda t, h: t.reshape(bsz, S, h, -1)
    qa = apply_rope(heads(qa, A_HEADS), cos_h, sin_h)
    ka = apply_rope(ka, cos_h, sin_h)
    qi = apply_rope(heads(qi, IDX_HEADS), cos_i, sin_i)
    ki = apply_rope(layer_norm(ki, ik_g, ik_b), cos_i, sin_i)
    o_a = dsa_attention(qa, ka, va, qi, ki, wi)
    o_b = dilated_attention(apply_rope(heads(qb, B_HEADS), cos_h, sin_h),
                            apply_rope(heads(kb, B_HEADS), cos_h, sin_h), heads(vb, B_HEADS))
    o_c = moba_attention(apply_rope(heads(qc, C_HEADS), cos_h, sin_h),
                         apply_rope(heads(kc, C_HEADS), cos_h, sin_h), heads(vc, C_HEADS))
    merged = (jax.nn.sigmoid(ga) * jnp.einsum('bsk,kd->bsd', o_a, w_br_a)
              + jax.nn.sigmoid(gb) * jnp.einsum('bsk,kd->bsd', o_b, w_br_b)
              + jax.nn.sigmoid(gc) * jnp.einsum('bsk,kd->bsd', o_c, w_br_c))
    return jnp.einsum('bsd,de->bse', merged, w_out)


def moe_ffn(u, w_router, router_bias, w_gate, w_up, w_down, ws_gate, ws_up, ws_down):
    bsz, S, _ = u.shape
    scores = jax.nn.sigmoid(jnp.einsum('bsd,de->bse', u, w_router, preferred_element_type=F32))
    biased = scores + router_bias.astype(F32)
    grp = biased.reshape(bsz, S, N_GROUPS, N_EXPERTS // N_GROUPS)
    grp_score = lax.top_k(grp, 2)[0].sum(-1)
    _, gidx = lax.top_k(grp_score, TOPK_GROUPS)
    gmask = jax.nn.one_hot(gidx, N_GROUPS, dtype=F32).sum(-2) > 0
    emask = jnp.repeat(gmask, N_EXPERTS // N_GROUPS, axis=-1)
    _, eidx = lax.top_k(jnp.where(emask, biased, -jnp.inf), TOP_K)
    w = jnp.take_along_axis(scores, eidx, -1)
    w = w / w.sum(-1, keepdims=True) * ROUTED_SCALE
    gates = (jax.nn.one_hot(eidx, N_EXPERTS, dtype=F32) * w[..., None]).sum(-2)
    def per_seq(args):
        us, gs = args
        h = (jax.nn.silu(jnp.einsum('sd,edf->sef', us, w_gate))
             * jnp.einsum('sd,edf->sef', us, w_up)) * gs[..., None].astype(us.dtype)
        return jnp.einsum('sef,efd->sd', h, w_down)
    routed = lax.map(per_seq, (u, gates))
    shared = jnp.einsum('bsf,fd->bsd', jax.nn.silu(jnp.einsum('bsd,df->bsf', u, ws_gate))
                        * jnp.einsum('bsd,df->bsf', u, ws_up), ws_down)
    return routed + shared


def setup_inputs(seed: int = 0) -> dict:
    key = jax.random.key(seed)
    ks = jax.random.split(key, 24)
    D, L = D_MODEL, DEPTH
    nrm = lambda k, shape, s: jax.random.normal(k, shape, F32) * s
    return {
        'x': nrm(ks[0], (BATCH, SEQ, D), 1.0),
        'c': nrm(ks[1], (BATCH, D), 1.0),
        'ada_w': nrm(ks[2], (L, D, 6 * D), 0.1 * D ** -0.5),
        'ada_b': nrm(ks[3], (L, 6 * D), 0.01),
        'w_in': nrm(ks[4], (L, D, IN_WIDTH), D ** -0.5),
        'ik_norm_g': 1.0 + nrm(ks[5], (L, IDX_DIM), 0.02),
        'ik_norm_b': nrm(ks[6], (L, IDX_DIM), 0.02),
        'w_br_a': nrm(ks[7], (L, A_HEADS * HEAD_DIM, D), (A_HEADS * HEAD_DIM) ** -0.5),
        'w_br_b': nrm(ks[8], (L, B_HEADS * HEAD_DIM, D), (B_HEADS * HEAD_DIM) ** -0.5),
        'w_br_c': nrm(ks[9], (L, C_HEADS * HEAD_DIM, D), (C_HEADS * HEAD_DIM) ** -0.5),
        'w_out': nrm(ks[10], (L, D, D), BETA * D ** -0.5),
        'ln1_g': 1.0 + nrm(ks[11], (L, D), 0.02),
        'ln1_b': nrm(ks[12], (L, D), 0.02),
        'w_router': nrm(ks[13], (L, D, N_EXPERTS), D ** -0.5),
        'router_bias': nrm(ks[14], (L, N_EXPERTS), 0.01),
        'w_gate': nrm(ks[15], (L, N_EXPERTS, D, EXPERT_DIM), D ** -0.5),
        'w_up': nrm(ks[16], (L, N_EXPERTS, D, EXPERT_DIM), D ** -0.5),
        'w_down': nrm(ks[17], (L, N_EXPERTS, EXPERT_DIM, D), BETA * EXPERT_DIM ** -0.5),
        'ws_gate': nrm(ks[18], (L, D, SHARED_DIM), D ** -0.5),
        'ws_up': nrm(ks[19], (L, D, SHARED_DIM), D ** -0.5),
        'ws_down': nrm(ks[20], (L, SHARED_DIM, D), BETA * SHARED_DIM ** -0.5),
        'ln2_g': 1.0 + nrm(ks[21], (L, D), 0.02),
        'ln2_b': nrm(ks[22], (L, D), 0.02),
    }


def reference(x, c, ada_w, ada_b, w_in, ik_norm_g, ik_norm_b, w_br_a, w_br_b, w_br_c, w_out,
              ln1_g, ln1_b, w_router, router_bias, w_gate, w_up, w_down, ws_gate, ws_up, ws_down,
              ln2_g, ln2_b):
    for l in range(DEPTH):
        mod = jnp.einsum('bd,dn->bn', jax.nn.silu(c), ada_w[l]) + ada_b[l]
        sh1, sc1, g1, sh2, sc2, g2 = jnp.split(mod[:, None, :], 6, axis=-1)
        u = x * (1 + sc1) + sh1
        y = token_mixing(u, w_in[l], ik_norm_g[l], ik_norm_b[l], w_br_a[l], w_br_b[l], w_br_c[l], w_out[l])
        x = layer_norm(ALPHA * x + (1 + g1) * y, ln1_g[l], ln1_b[l])
        u = x * (1 + sc2) + sh2
        y = moe_ffn(u, w_router[l], router_bias[l], w_gate[l], w_up[l], w_down[l],
                    ws_gate[l], ws_up[l], ws_down[l])
        x = layer_norm(ALPHA * x + (1 + g2) * y, ln2_g[l], ln2_b[l])
    return x
```

```python
import functools
import math

import numpy as np
import jax
import jax.numpy as jnp
from jax import lax
from jax.experimental import pallas as pl
from jax.experimental.pallas import tpu as pltpu

F32 = jnp.float32
BF16 = jnp.bfloat16

HEAD_DIM = 64
ROPE_THETA = 10000.0
LN_EPS = 1e-5
A_HEADS = 8
IDX_HEADS = 8
IDX_DIM = 32
DSA_TOPK = 256
B_HEADS = 8
B_GROUPS = ((128, 1), (512, 4), (2048, 16))
C_HEADS = 8
MOBA_BLOCK = 256
MOBA_TOPK = 3
N_EXPERTS = 64
N_GROUPS = 8
TOPK_GROUPS = 4
TOP_K = 8
EXPERT_DIM = 256
ROUTED_SCALE = 2.5
DEPTH = 2
ALPHA = (2 * DEPTH) ** 0.25

LANES = 128
NEG = -1e30
INT_MIN = -2147483648
VMEM_LIMIT = 56 * 1024 * 1024

TM_PROJ = 512
TQ_DSA = 256
T_ATT = 256
TM_MERGE = 512
TM_ROUTER = 512
TM_MOE = 1024

SEG = {}
_off = 0
for _name, _w in (("qa", 512), ("ka", 128), ("va", 128), ("qi", 256), ("ki", 128), ("wi", 128),
                  ("qb", 512), ("kb", 512), ("vb", 512), ("qc", 512), ("kc", 512), ("vc", 512)):
    SEG[_name] = (_off, _off + _w)
    _off += _w
PROJ_COLS = _off


def _cparams(sem):
    return pltpu.CompilerParams(dimension_semantics=sem, vmem_limit_bytes=VMEM_LIMIT)


def _sigmoid(x):
    return 1.0 / (1.0 + jnp.exp(-x))


def _dot_t(a, b):
    return lax.dot_general(a, b, (((1,), (1,)), ((), ())), preferred_element_type=F32)


def _lane_iota(shape):
    return lax.broadcasted_iota(jnp.int32, shape, len(shape) - 1)


def _mod_kernel(c_ref, w_ref, b_ref, o_ref):
    c = c_ref[...]
    sc = (c * _sigmoid(c)).astype(BF16)
    o_ref[...] = jnp.dot(sc, w_ref[...].astype(BF16), preferred_element_type=F32) + b_ref[...]


def _modulation(c, ada_w, ada_b):
    L, D, N = ada_w.shape
    Bsz = c.shape[0]
    tn = 1536
    return pl.pallas_call(
        _mod_kernel,
        out_shape=jax.ShapeDtypeStruct((L, Bsz, N), F32),
        grid=(L, N // tn),
        in_specs=[pl.BlockSpec((Bsz, D), lambda l, j: (0, 0)),
                  pl.BlockSpec((None, D, tn), lambda l, j: (l, 0, j)),
                  pl.BlockSpec((None, 1, tn), lambda l, j: (l, 0, j))],
        out_specs=pl.BlockSpec((None, Bsz, tn), lambda l, j: (l, 0, j)),
        compiler_params=_cparams(("arbitrary", "arbitrary")),
        name="adaln_mod",
    )(c, ada_w, ada_b.reshape(L, 1, N))


def _swap_half(p, half):
    lane = _lane_iota(p.shape)
    first = (lane & (2 * half - 1)) < half
    return jnp.where(first, pltpu.roll(p, LANES - half, 1), pltpu.roll(p, half, 1))


def _rope_chunks(p, cos, sin, half):
    outs = []
    for c in range(p.shape[1] // LANES):
        pc = p[:, c * LANES:(c + 1) * LANES]
        outs.append(pc * cos + _swap_half(pc, half) * sin)
    return outs


def _proj_kernel(x_ref, mod_ref, w_ref, cos64_ref, sin64_ref, cos32_ref, sin32_ref, ikg_ref, ikb_ref,
                 qa_ref, ka_ref, va_ref, qi_ref, ki_ref, wi_ref,
                 qb_ref, kb_ref, vb_ref, qc_ref, kc_ref, vc_ref):
    x = x_ref[...]
    sh1 = mod_ref[0:1, :]
    sc1 = mod_ref[1:2, :]
    u = (x * (1.0 + sc1) + sh1).astype(BF16)

    def mm(name):
        lo, hi = SEG[name]
        return jnp.dot(u, w_ref[:, lo:hi], preferred_element_type=F32)

    cos64, sin64 = cos64_ref[...], sin64_ref[...]
    cos32, sin32 = cos32_ref[...], sin32_ref[...]
    qscale = HEAD_DIM ** -0.5

    def store_rope64(name, ref, scale):
        chunks = _rope_chunks(mm(name), cos64, sin64, HEAD_DIM // 2)
        for c, ch in enumerate(chunks):
            ref[:, c * LANES:(c + 1) * LANES] = (ch * scale).astype(ref.dtype)

    store_rope64("qa", qa_ref, qscale)
    store_rope64("ka", ka_ref, 1.0)
    va_ref[...] = mm("va").astype(va_ref.dtype)
    for c, ch in enumerate(_rope_chunks(mm("qi"), cos32, sin32, IDX_DIM // 2)):
        qi_ref[:, c * LANES:(c + 1) * LANES] = ch.astype(qi_ref.dtype)
    pk = mm("ki")
    lane = _lane_iota((1, LANES))
    real = (lane < IDX_DIM).astype(F32)
    mu = jnp.sum(pk, axis=1, keepdims=True) * (1.0 / IDX_DIM)
    xc = (pk - mu) * real
    var = jnp.sum(xc * xc, axis=1, keepdims=True) * (1.0 / IDX_DIM)
    kn = xc * lax.rsqrt(var + LN_EPS) * ikg_ref[...] + ikb_ref[...]
    kr = kn * cos32 + _swap_half(kn, IDX_DIM // 2) * sin32
    kr = kr * real
    kr = kr + pltpu.roll(kr, 32, 1) + pltpu.roll(kr, 64, 1) + pltpu.roll(kr, 96, 1)
    ki_ref[...] = kr.astype(ki_ref.dtype)
    wi_ref[...] = mm("wi") * (IDX_HEADS ** -0.5 * IDX_DIM ** -0.5)
    store_rope64("qb", qb_ref, qscale)
    store_rope64("kb", kb_ref, 1.0)
    vb_ref[...] = mm("vb").astype(vb_ref.dtype)
    store_rope64("qc", qc_ref, qscale)
    store_rope64("kc", kc_ref, 1.0)
    vc_ref[...] = mm("vc").astype(vc_ref.dtype)


def _pack_w_in(w_in_l):
    D = w_in_l.shape[0]
    o = 0
    parts = {}
    for name, w in (("qa", 512), ("ka", 64), ("va", 64), ("qi", 256), ("ki", 32), ("wi", 8),
                    ("qb", 512), ("kb", 512), ("vb", 512), ("qc", 512), ("kc", 512), ("vc", 512),
                    ("ga", 1024), ("gb", 1024), ("gc", 1024)):
        parts[name] = w_in_l[:, o:o + w]
        o += w
    z = lambda n: jnp.zeros((D, n), w_in_l.dtype)
    packed = jnp.concatenate([
        parts["qa"], parts["ka"], parts["ka"], parts["va"], parts["va"], parts["qi"],
        parts["ki"], z(LANES - IDX_DIM), parts["wi"], z(LANES - IDX_HEADS),
        parts["qb"], parts["kb"], parts["vb"], parts["qc"], parts["kc"], parts["vc"]], axis=1)
    gates = jnp.concatenate([parts["ga"], parts["gb"], parts["gc"]], axis=1)
    return packed.astype(BF16), gates.astype(BF16)


def _rope_tables(S):
    def tab(dim):
        inv = ROPE_THETA ** (-jnp.arange(0, dim, 2, dtype=F32) / dim)
        ang = jnp.arange(S, dtype=F32)[:, None] * inv[None, :]
        cos, sin = jnp.cos(ang), jnp.sin(ang)
        reps = LANES // dim
        cosf = jnp.tile(jnp.concatenate([cos, cos], axis=1), (1, reps))
        sinf = jnp.tile(jnp.concatenate([-sin, sin], axis=1), (1, reps))
        return cosf, sinf
    return tab(HEAD_DIM) + tab(IDX_DIM)


def _in_projection(x, mod3, w_packed, tables, ikg, ikb):
    Bsz, S, D = x.shape
    tm = min(TM_PROJ, S)
    cos64, sin64, cos32, sin32 = tables
    tok = lambda w: pl.BlockSpec((None, tm, w), lambda b, i: (b, i, 0))
    tab = pl.BlockSpec((tm, LANES), lambda b, i: (i, 0))
    row = pl.BlockSpec((1, LANES), lambda b, i: (0, 0))
    widths = [512, 128, 128, 256, 128, 128, 512, 512, 512, 512, 512, 512]
    dtypes = [BF16, BF16, BF16, BF16, BF16, F32, BF16, BF16, BF16, BF16, BF16, BF16]
    return pl.pallas_call(
        _proj_kernel,
        out_shape=[jax.ShapeDtypeStruct((Bsz, S, w), dt) for w, dt in zip(widths, dtypes)],
        grid=(Bsz, S // tm),
        in_specs=[tok(D),
                  pl.BlockSpec((None, 6, D), lambda b, i: (b, 0, 0)),
                  pl.BlockSpec((D, PROJ_COLS), lambda b, i: (0, 0)),
                  tab, tab, tab, tab, row, row],
        out_specs=[tok(w) for w in widths],
        compiler_params=_cparams(("arbitrary", "arbitrary")),
        name="in_proj",
    )(x, mod3, w_packed, cos64, sin64, cos32, sin32, ikg, ikb)


def _dsa_kernel(qa_ref, ka_ref, va_ref, qi_ref, ki_ref, wi_ref, tri_ref, o_ref, key_ref, bias_ref, *, topk):
    tq = qa_ref.shape[0]
    S = ka_ref.shape[0]
    q0 = pl.program_id(1) * tq
    lane = _lane_iota((1, LANES))

    kir = ki_ref[...]
    score = jnp.zeros((tq, S), F32)
    for h in range(IDX_HEADS):
        c, r = divmod(h, LANES // IDX_DIM)
        hm = ((lane >> 5) == r).astype(BF16)
        d = _dot_t(qi_ref[:, c * LANES:(c + 1) * LANES] * hm, kir)
        score = score + wi_ref[:, h:h + 1] * jnp.maximum(d, 0.0)
    score = score + 0.0

    col = lax.broadcasted_iota(jnp.int32, (tq, S), 1)
    rowi = q0 + lax.broadcasted_iota(jnp.int32, (tq, S), 0)
    causal = col <= rowi
    bits = pltpu.bitcast(score, jnp.int32)
    key = jnp.where(bits < 0, bits ^ 0x7FFFFFFF, bits)
    key_ref[...] = jnp.where(causal, key, INT_MIN)

    def count_ge(t):
        return jnp.sum(jnp.where(key_ref[...] >= t, 1.0, 0.0), axis=1, keepdims=True)

    kf = float(topk)
    prefix = jnp.where(count_ge(jnp.zeros((tq, 1), jnp.int32)) >= kf, 0, INT_MIN).astype(jnp.int32)

    def bit_step(i, prefix):
        cand = prefix | jnp.left_shift(jnp.int32(1), 30 - i)
        return jnp.where(count_ge(cand) >= kf, cand, prefix)

    thr = lax.fori_loop(0, 31, bit_step, prefix)

    need = kf - jnp.sum(jnp.where(key_ref[...] > thr, 1.0, 0.0), axis=1, keepdims=True)
    tri = tri_ref[...]
    cw = tri.shape[0]
    carry = jnp.zeros((tq, 1), F32)
    for c in range(S // cw):
        sl = slice(c * cw, (c + 1) * cw)
        keyc = key_ref[:, sl]
        colc = c * cw + lax.broadcasted_iota(jnp.int32, (tq, cw), 1)
        rowc = q0 + lax.broadcasted_iota(jnp.int32, (tq, cw), 0)
        eqc = jnp.where(keyc == thr, 1.0, 0.0)
        inc = jnp.dot(eqc.astype(BF16), tri, preferred_element_type=F32)
        tie_ok = jnp.where(carry + inc - eqc < need, eqc, 0.0)
        picked = jnp.where(keyc > thr, 1.0, tie_ok)
        bias_ref[:, sl] = jnp.where((picked > 0.0) & (colc <= rowc), 0.0, NEG)
        carry = carry + inc[:, cw - 1:cw]

    kar = ka_ref[...]
    var = va_ref[...]
    for c in range(A_HEADS // 2):
        outc = jnp.zeros((tq, LANES), F32)
        for r in range(2):
            hm = (lane >> 6) == r
            s = _dot_t(qa_ref[:, c * LANES:(c + 1) * LANES] * hm.astype(BF16), kar) + bias_ref[...]
            m = jnp.max(s, axis=1, keepdims=True)
            p = jnp.exp(s - m)
            l = jnp.sum(p, axis=1, keepdims=True)
            o = jnp.dot(p.astype(BF16), var, preferred_element_type=F32) / l
            outc = outc + o * hm.astype(F32)
        o_ref[:, c * LANES:(c + 1) * LANES] = outc.astype(o_ref.dtype)


def _tri_inclusive(n):
    i = np.arange(n)
    return jnp.asarray((i[:, None] <= i[None, :]).astype(np.float32), dtype=BF16)


def _dsa_attention(qa, ka, va, qi, ki, wi):
    Bsz, S, _ = qa.shape
    tq = min(TQ_DSA, S)
    topk = min(DSA_TOPK, S // 4)
    cw = min(256, S)
    qspec = lambda w: pl.BlockSpec((None, tq, w), lambda b, i: (b, i, 0))
    kspec = pl.BlockSpec((None, S, LANES), lambda b, i: (b, 0, 0))
    return pl.pallas_call(
        functools.partial(_dsa_kernel, topk=topk),
        out_shape=jax.ShapeDtypeStruct((Bsz, S, A_HEADS * HEAD_DIM), BF16),
        grid=(Bsz, S // tq),
        in_specs=[qspec(512), kspec, kspec, qspec(256), kspec, qspec(LANES),
                  pl.BlockSpec((cw, cw), lambda b, i: (0, 0))],
        out_specs=qspec(512),
        scratch_shapes=[pltpu.VMEM((tq, S), jnp.int32), pltpu.VMEM((tq, S), F32)],
        compiler_params=_cparams(("arbitrary", "arbitrary")),
        name="dsa_attention",
    )(qa, ka, va, qi, ki, wi, _tri_inclusive(cw))


def _flash_step(qm, k, v, bias, m, l, acc):
    s = _dot_t(qm, k) + bias
    mn = jnp.maximum(m, jnp.max(s, axis=1, keepdims=True))
    a = jnp.exp(m - mn)
    p = jnp.exp(s - mn)
    l = a * l + jnp.sum(p, axis=1, keepdims=True)
    acc = a * acc + jnp.dot(p.astype(BF16), v, preferred_element_type=F32)
    return mn, l, acc


def _dilated_kernel(q_ref, k_ref, v_ref, bias_ref, o_ref):
    t = q_ref.shape[0]
    qb = pl.program_id(2)
    lane = _lane_iota((1, LANES))
    q = q_ref[...]
    out = jnp.zeros((t, LANES), F32)
    for r in range(2):
        hm = (lane >> 6) == r
        qm = q * hm.astype(BF16)

        def body(j, carry):
            start = pl.multiple_of((qb - j) * t, t)
            return _flash_step(qm, k_ref[pl.ds(start, t), :], v_ref[pl.ds(start, t), :],
                               bias_ref[j], *carry)

        init = (jnp.full((t, 1), NEG, F32), jnp.zeros((t, 1), F32), jnp.zeros((t, LANES), F32))
        m, l, acc = lax.fori_loop(0, qb + 1, body, init)
        out = out + (acc / l) * hm.astype(F32)
    o_ref[...] = out.astype(o_ref.dtype)


def _dilated_bias(S, t):
    nd = S // t
    i = np.arange(t)
    tabs = []
    for d in range(nd):
        delta = d * t + i[:, None] - i[None, :]
        cnt = np.zeros((t, t), np.float64)
        for window, dil in B_GROUPS:
            cnt += (delta >= 0) & (delta % dil == 0) & (delta <= window)
        tabs.append(np.where(cnt > 0, np.log(np.maximum(cnt, 1.0)), NEG))
    return jnp.asarray(np.stack(tabs).astype(np.float32))


def _pair_attention(kernel_fn, q, k, v, *extra, extra_specs, name):
    Bsz, S, W = q.shape
    t = min(T_ATT, S)
    qspec = pl.BlockSpec((None, t, LANES), lambda b, p, i: (b, i, p))
    kspec = pl.BlockSpec((None, S, LANES), lambda b, p, i: (b, 0, p))
    return pl.pallas_call(
        kernel_fn,
        out_shape=jax.ShapeDtypeStruct((Bsz, S, W), BF16),
        grid=(Bsz, W // LANES, S // t),
        in_specs=[qspec, kspec, kspec] + list(extra_specs),
        out_specs=qspec,
        compiler_params=_cparams(("arbitrary", "arbitrary", "arbitrary")),
        name=name,
    )(q, k, v, *extra)


def _dilated_attention(q, k, v):
    S = q.shape[1]
    t = min(T_ATT, S)
    bias = _dilated_bias(S, t)
    spec = pl.BlockSpec(bias.shape, lambda b, p, i: (0, 0, 0))
    return _pair_attention(_dilated_kernel, q, k, v, bias, extra_specs=[spec], name="dilated_attention")


def _moba_kernel(q_ref, k_ref, v_ref, blk_ref, tril_ref, o_ref, *, n_sel, nb):
    t = q_ref.shape[0]
    qb = pl.program_id(2)
    lane = _lane_iota((1, LANES))
    blane = lane
    q = q_ref[...]
    kmean = (jnp.dot(blk_ref[...], k_ref[...], preferred_element_type=F32) * (1.0 / t)).astype(BF16)
    out = jnp.zeros((t, LANES), F32)
    for r in range(2):
        hm = (lane >> 6) == r
        qm = q * hm.astype(BF16)
        gate = _dot_t(qm, kmean)
        rank = jnp.zeros((t, LANES), F32)
        for mb in range(nb):
            gm = gate[:, mb:mb + 1]
            lower = (blane > mb).astype(F32)
            ahead = jnp.where(gm > gate, 1.0, 0.0) + jnp.where(gm == gate, 1.0, 0.0) * lower
            past = jnp.where(mb < qb, 1.0, 0.0)
            rank = rank + ahead * past
        selbias = jnp.where(rank < float(n_sel), 0.0, NEG) + jnp.where(blane < qb, 0.0, NEG)

        start = pl.multiple_of(qb * t, t)
        init = _flash_step(qm, k_ref[pl.ds(start, t), :], v_ref[pl.ds(start, t), :], tril_ref[...],
                           jnp.full((t, 1), NEG, F32), jnp.zeros((t, 1), F32), jnp.zeros((t, LANES), F32))

        def body(kb, carry):
            st = pl.multiple_of(kb * t, t)
            b = jnp.sum(selbias * (blane == kb).astype(F32), axis=1, keepdims=True)
            return _flash_step(qm, k_ref[pl.ds(st, t), :], v_ref[pl.ds(st, t), :], b, *carry)

        m, l, acc = lax.fori_loop(0, qb, body, init)
        out = out + (acc / l) * hm.astype(F32)
    o_ref[...] = out.astype(o_ref.dtype)


def _moba_attention(q, k, v):
    S = q.shape[1]
    assert S % MOBA_BLOCK == 0 and T_ATT == MOBA_BLOCK
    t = MOBA_BLOCK
    nb = S // t
    n_sel = max(1, min(MOBA_TOPK, nb - 1))
    assert nb <= LANES
    blk = jnp.asarray((np.arange(S)[None, :] // t == np.arange(LANES)[:, None]).astype(np.float32), dtype=BF16)
    i = np.arange(t)
    tril = jnp.asarray(np.where(i[None, :] <= i[:, None], 0.0, NEG).astype(np.float32))
    specs = [pl.BlockSpec((LANES, S), lambda b, p, i: (0, 0)), pl.BlockSpec((t, t), lambda b, p, i: (0, 0))]
    return _pair_attention(functools.partial(_moba_kernel, n_sel=n_sel, nb=nb), q, k, v, blk, tril,
                           extra_specs=specs, name="moba_attention")


def _layer_norm(z, g, b):
    mu = jnp.mean(z, axis=1, keepdims=True)
    zc = z - mu
    var = jnp.mean(zc * zc, axis=1, keepdims=True)
    return zc * lax.rsqrt(var + LN_EPS) * g + b


def _merge_kernel(x_ref, mod_ref, oa_ref, ob_ref, oc_ref, wg_ref, wa_ref, wb_ref, wc_ref, wo_ref,
                  g_ref, b_ref, x1_ref, u2_ref):
    x = x_ref[...]
    D = x.shape[1]
    sh1, sc1, g1 = mod_ref[0:1, :], mod_ref[1:2, :], mod_ref[2:3, :]
    sh2, sc2 = mod_ref[3:4, :], mod_ref[4:5, :]
    u = (x * (1.0 + sc1) + sh1).astype(BF16)
    merged = jnp.zeros_like(x)
    for j, (o_ref, w_ref) in enumerate(((oa_ref, wa_ref), (ob_ref, wb_ref), (oc_ref, wc_ref))):
        gate = _sigmoid(jnp.dot(u, wg_ref[:, j * D:(j + 1) * D], preferred_element_type=F32))
        merged = merged + gate * jnp.dot(o_ref[...], w_ref[...], preferred_element_type=F32)
    y = jnp.dot(merged.astype(BF16), wo_ref[...], preferred_element_type=F32)
    x1 = _layer_norm(ALPHA * x + (1.0 + g1) * y, g_ref[...], b_ref[...])
    x1_ref[...] = x1
    u2_ref[...] = (x1 * (1.0 + sc2) + sh2).astype(BF16)


def _merge(x, mod3, oa, ob, oc, w_gates, wa, wb, wc, wo, ln_g, ln_b):
    Bsz, S, D = x.shape
    tm = min(TM_MERGE, S)
    tok = lambda w: pl.BlockSpec((None, tm, w), lambda b, i: (b, i, 0))
    full = lambda a: pl.BlockSpec(a.shape, lambda b, i: (0,) * a.ndim)
    return pl.pallas_call(
        _merge_kernel,
        out_shape=[jax.ShapeDtypeStruct((Bsz, S, D), F32), jax.ShapeDtypeStruct((Bsz, S, D), BF16)],
        grid=(Bsz, S // tm),
        in_specs=[tok(D), pl.BlockSpec((None, 6, D), lambda b, i: (b, 0, 0)),
                  tok(oa.shape[2]), tok(ob.shape[2]), tok(oc.shape[2]),
                  full(w_gates), full(wa), full(wb), full(wc), full(wo), full(ln_g), full(ln_b)],
        out_specs=[tok(D), tok(D)],
        compiler_params=_cparams(("arbitrary", "arbitrary")),
        name="merge_ln1",
    )(x, mod3, oa, ob, oc, w_gates, wa, wb, wc, wo, ln_g, ln_b)


def _router_kernel(u_ref, wr_ref, rb_ref, g_ref):
    tm = u_ref.shape[0]
    gsz = N_EXPERTS // N_GROUPS
    scores = _sigmoid(_dot_t(wr_ref[...], u_ref[...]))
    biased = scores + rb_ref[...]
    gscore = []
    for g in range(N_GROUPS):
        vg = biased[g * gsz:(g + 1) * gsz, :]
        m1 = jnp.max(vg, axis=0, keepdims=True)
        is1 = vg == m1
        n1 = jnp.sum(jnp.where(is1, 1.0, 0.0), axis=0, keepdims=True)
        m2 = jnp.max(jnp.where(is1, -jnp.inf, vg), axis=0, keepdims=True)
        gscore.append(m1 + jnp.where(n1 >= 2.0, m1, m2))
    masked = []
    for g in range(N_GROUPS):
        rank = jnp.zeros((1, tm), F32)
        for o in range(N_GROUPS):
            if o == g:
                continue
            ahead = (gscore[o] >= gscore[g]) if o < g else (gscore[o] > gscore[g])
            rank = rank + jnp.where(ahead, 1.0, 0.0)
        keep = rank < float(TOPK_GROUPS)
        masked.append(jnp.where(keep, biased[g * gsz:(g + 1) * gsz, :], -jnp.inf))
    mb = jnp.concatenate(masked, axis=0)
    eidx = lax.broadcasted_iota(jnp.int32, (N_EXPERTS, 1), 0)
    rank = jnp.zeros((N_EXPERTS, tm), F32)
    for o in range(N_EXPERTS):
        vo = mb[o:o + 1, :]
        lower = (eidx > o).astype(F32)
        rank = rank + jnp.where(vo > mb, 1.0, 0.0) + jnp.where(vo == mb, 1.0, 0.0) * lower
    w = jnp.where(rank < float(TOP_K), scores, 0.0)
    gates = w / jnp.sum(w, axis=0, keepdims=True) * ROUTED_SCALE
    sub = lax.broadcasted_iota(jnp.int32, (LANES - N_EXPERTS, tm), 0)
    gt = jnp.concatenate([gates, jnp.where(sub == 0, 1.0, 0.0)], axis=0)
    g_ref[...] = gt.T


def _router(u2, w_router_t, router_bias):
    N, D = u2.shape
    tm = min(TM_ROUTER, N)
    return pl.pallas_call(
        _router_kernel,
        out_shape=jax.ShapeDtypeStruct((N, LANES), F32),
        grid=(N // tm,),
        in_specs=[pl.BlockSpec((tm, D), lambda i: (i, 0)),
                  pl.BlockSpec((N_EXPERTS, D), lambda i: (0, 0)),
                  pl.BlockSpec((N_EXPERTS, 1), lambda i: (0, 0))],
        out_specs=pl.BlockSpec((tm, LANES), lambda i: (i, 0)),
        compiler_params=_cparams(("arbitrary",)),
        name="moe_router",
    )(u2, w_router_t, router_bias)


def _moe_kernel(u_ref, gates_ref, x1_ref, mod_ref, wgu_ref, wd_ref, g_ref, b_ref, o_ref, acc_ref):
    e = pl.program_id(1)

    @pl.when(e == 0)
    def _():
        acc_ref[...] = jnp.zeros_like(acc_ref)

    hgu = jnp.dot(u_ref[...], wgu_ref[...], preferred_element_type=F32)
    hg, hu = hgu[:, :EXPERT_DIM], hgu[:, EXPERT_DIM:]
    lane = _lane_iota((1, LANES))
    gate = jnp.sum(jnp.where(lane == e, gates_ref[...], 0.0), axis=1, keepdims=True)
    h = (hg * _sigmoid(hg)) * hu * gate
    acc_ref[...] += jnp.dot(h.astype(BF16), wd_ref[...], preferred_element_type=F32)

    @pl.when(e == pl.num_programs(1) - 1)
    def _():
        g2 = mod_ref[5:6, :]
        o_ref[...] = _layer_norm(ALPHA * x1_ref[...] + (1.0 + g2) * acc_ref[...], g_ref[...], b_ref[...])


def _moe(u2, gates, x1, mod3, wgu, wd, ln_g, ln_b, seq):
    N, D = u2.shape
    tm = min(TM_MOE, seq)
    per_seq = seq // tm
    ne = wgu.shape[0]
    tok = lambda w: pl.BlockSpec((tm, w), lambda i, e: (i, 0))
    return pl.pallas_call(
        _moe_kernel,
        out_shape=jax.ShapeDtypeStruct((N, D), F32),
        grid=(N // tm, ne),
        in_specs=[tok(D), tok(LANES), tok(D),
                  pl.BlockSpec((None, 6, D), lambda i, e: (i // per_seq, 0, 0)),
                  pl.BlockSpec((None, D, 2 * EXPERT_DIM), lambda i, e: (e, 0, 0)),
                  pl.BlockSpec((None, EXPERT_DIM, D), lambda i, e: (e, 0, 0)),
                  pl.BlockSpec((1, D), lambda i, e: (0, 0)),
                  pl.BlockSpec((1, D), lambda i, e: (0, 0))],
        out_specs=tok(D),
        scratch_shapes=[pltpu.VMEM((tm, D), F32)],
        compiler_params=_cparams(("arbitrary", "arbitrary")),
        name="moe_experts_ln2",
    )(u2, gates, x1, mod3, wgu, wd, ln_g, ln_b)


def kernel(x, c, ada_w, ada_b, w_in, ik_norm_g, ik_norm_b, w_br_a, w_br_b, w_br_c, w_out, ln1_g, ln1_b,
           w_router, router_bias, w_gate, w_up, w_down, ws_gate, ws_up, ws_down, ln2_g, ln2_b):
    Bsz, S, D = x.shape
    L = ada_w.shape[0]
    mod = _modulation(c, ada_w, ada_b).reshape(L, Bsz, 6, D)
    tables = _rope_tables(S)
    pad = lambda v: jnp.concatenate([v, jnp.zeros((LANES - IDX_DIM,), v.dtype)]).reshape(1, LANES)
    for l in range(L):
        w_packed, w_gates = _pack_w_in(w_in[l])
        qa, ka, va, qi, ki, wi, qb, kb, vb, qc, kc, vc = _in_projection(
            x, mod[l], w_packed, tables, pad(ik_norm_g[l]), pad(ik_norm_b[l]))
        oa = _dsa_attention(qa, ka, va, qi, ki, wi)
        ob = _dilated_attention(qb, kb, vb)
        oc = _moba_attention(qc, kc, vc)
        x1, u2 = _merge(x, mod[l], oa, ob, oc, w_gates, w_br_a[l].astype(BF16), w_br_b[l].astype(BF16),
                        w_br_c[l].astype(BF16), w_out[l].astype(BF16),
                        ln1_g[l].reshape(1, D), ln1_b[l].reshape(1, D))
        u2f = u2.reshape(Bsz * S, D)
        gates = _router(u2f, w_router[l].T.astype(BF16), router_bias[l].reshape(N_EXPERTS, 1))
        wgu = jnp.concatenate([jnp.concatenate([w_gate[l], w_up[l]], axis=2),
                               jnp.concatenate([ws_gate[l], ws_up[l]], axis=1)[None]], axis=0).astype(BF16)
        wd = jnp.concatenate([w_down[l], ws_down[l][None]], axis=0).astype(BF16)
        x = _moe(u2f, gates, x1.reshape(Bsz * S, D), mod[l], wgu, wd,
                 ln2_g[l].reshape(1, D), ln2_b[l].reshape(1, D), S).reshape(Bsz, S, D)
    return x
```
